```python
import jax, jax.numpy as jnp
from jax import lax
import numpy as np

D_MODEL = 1024
BATCH = 4
SEQ = 8192
DEPTH = 1

D_CONV = D_MODEL
CONV_GROUPS = 8
CONV_WIDTH = 3
D_RNN = (5 * D_MODEL) // 4
RNN_BLOCKS = 10
RNN_BLOCK = D_RNN // RNN_BLOCKS
RNN_CONV_WIDTH = 4
LRU_C = 8.0
W_IN_COLS = 3 * D_CONV + 2 * D_RNN + 2 * D_MODEL
SPLITS = (D_CONV, 2 * D_CONV, 3 * D_CONV, 3 * D_CONV + D_RNN,
          3 * D_CONV + 2 * D_RNN, 3 * D_CONV + 2 * D_RNN + D_MODEL)
PEER_HEADS = 8
PEER_NKEYS = 128
PEER_NEXPERTS = PEER_NKEYS * PEER_NKEYS
PEER_QDIM = 256
PEER_HALF = PEER_QDIM // 2
PEER_TOPK = 16
PEER_CHUNK = 128
EPS = 1e-6

kernel_name = "hybrid_conv_rglru_peer"


def rms_norm(x, g):
    xf = x.astype(jnp.float32)
    y = xf * lax.rsqrt(jnp.mean(xf * xf, axis=-1, keepdims=True) + EPS)
    return (y * g.astype(jnp.float32)).astype(x.dtype)


def causal_depthwise_conv(x, w):
    k = w.shape[0]
    s = x.shape[1]
    xp = jnp.pad(x, ((0, 0), (k - 1, 0), (0, 0)))
    out = xp[:, 0:s] * w[0]
    for j in range(1, k):
        out = out + xp[:, j:j + s] * w[j]
    return out


def block_diag_linear(x, w, b):
    bsz, s, _ = x.shape
    xb = x.reshape(bsz, s, RNN_BLOCKS, RNN_BLOCK)
    return jnp.einsum('bsni,nij->bsnj', xb, w).reshape(bsz, s, D_RNN) + b


def rg_lru(x, w_rg, b_rg, w_ig, b_ig, lam):
    r = jax.nn.sigmoid(block_diag_linear(x, w_rg, b_rg).astype(jnp.float32))
    i = jax.nn.sigmoid(block_diag_linear(x, w_ig, b_ig).astype(jnp.float32))
    log_a = -LRU_C * r * jax.nn.softplus(-lam.astype(jnp.float32))
    a = jnp.exp(log_a)
    mult = jnp.sqrt(jnp.maximum(-jnp.expm1(2.0 * log_a), 0.0))
    b = mult * i * x.astype(jnp.float32)

    def combine(c1, c2):
        a1, b1 = c1
        a2, b2 = c2
        return a1 * a2, a2 * b1 + b2

    _, h = lax.associative_scan(combine, (a, b), axis=1)
    return h.astype(x.dtype)


def hybrid_mixer(n, w_in, conv_a_w, w_out_a, rnn_conv_w, rnn_conv_b,
                 w_rg, b_rg, w_ig, b_ig, lru_lambda, w_out_b, w_o):
    proj = n @ w_in
    a_x, a_b, a_c, r_x, r_y, g_a, g_b = jnp.split(proj, SPLITS, axis=-1)
    y_a = (a_b * causal_depthwise_conv(a_c * a_x, conv_a_w)) @ w_out_a
    r_c = causal_depthwise_conv(r_x, rnn_conv_w) + rnn_conv_b
    r_h = rg_lru(r_c, w_rg, b_rg, w_ig, b_ig, lru_lambda)
    y_b = (jax.nn.gelu(r_y, approximate=True) * r_h) @ w_out_b
    merged = jax.nn.sigmoid(g_a) * y_a + jax.nn.sigmoid(g_b) * y_b
    return merged @ w_o


def peer(n, w_q, sub_keys_1, sub_keys_2, expert_u, expert_v):
    bsz, s, d = n.shape
    xt = n.reshape(-1, PEER_CHUNK, d)

    def chunk(xc):
        q = (xc @ w_q).reshape(PEER_CHUNK, PEER_HEADS, 2, PEER_HALF)
        s1 = jnp.einsum('thd,hnd->thn', q[:, :, 0], sub_keys_1).astype(jnp.float32)
        s2 = jnp.einsum('thd,hnd->thn', q[:, :, 1], sub_keys_2).astype(jnp.float32)
        v1, i1 = lax.top_k(s1, PEER_TOPK)
        v2, i2 = lax.top_k(s2, PEER_TOPK)
        cand_s = (v1[..., :, None] + v2[..., None, :]).reshape(PEER_CHUNK, PEER_HEADS, PEER_TOPK * PEER_TOPK)
        cand_i = (i1[..., :, None] * PEER_NKEYS + i2[..., None, :]).reshape(PEER_CHUNK, PEER_HEADS, PEER_TOPK * PEER_TOPK)
        top_s, pos = lax.top_k(cand_s, PEER_TOPK)
        idx = jnp.take_along_axis(cand_i, pos, axis=-1)
        g = jax.nn.softmax(top_s, axis=-1)
        u_sel = jnp.take(expert_u, idx, axis=0)
        act = jax.nn.gelu(jnp.einsum('thkd,td->thk', u_sel, xc).astype(jnp.float32))
        v_sel = jnp.take(expert_v, idx, axis=0)
        return jnp.einsum('thk,thkd->td', (g * act).astype(xc.dtype), v_sel)

    return lax.map(chunk, xt).reshape(bsz, s, d)


def setup_inputs(seed: int = 0) -> dict:
    key = jax.random.key(seed)
    ks = jax.random.split(key, 24)
    f = jnp.float32
    L = DEPTH

    def nrm(k, shape, scale):
        return jax.random.normal(k, shape, f) * scale

    u = jax.random.uniform(ks[10], (L, D_RNN), f, 0.9, 0.999)
    a0 = u ** (1.0 / LRU_C)
    lru_lambda = jnp.log(a0) - jnp.log1p(-a0)
    return {
        "x": nrm(ks[0], (BATCH, SEQ, D_MODEL), 1.0),
        "norm_mix": 1.0 + nrm(ks[1], (L, D_MODEL), 0.01),
        "w_in": nrm(ks[2], (L, D_MODEL, W_IN_COLS), D_MODEL ** -0.5),
        "conv_a_w": nrm(ks[3], (L, CONV_WIDTH, D_CONV), CONV_WIDTH ** -0.5),
        "w_out_a": nrm(ks[4], (L, D_CONV, D_MODEL), D_CONV ** -0.5),
        "rnn_conv_w": nrm(ks[5], (L, RNN_CONV_WIDTH, D_RNN), RNN_CONV_WIDTH ** -0.5),
        "rnn_conv_b": nrm(ks[6], (L, D_RNN), 0.01),
        "w_rg": nrm(ks[7], (L, RNN_BLOCKS, RNN_BLOCK, RNN_BLOCK), RNN_BLOCK ** -0.5),
        "b_rg": nrm(ks[8], (L, D_RNN), 0.01),
        "w_ig": nrm(ks[9], (L, RNN_BLOCKS, RNN_BLOCK, RNN_BLOCK), RNN_BLOCK ** -0.5),
        "b_ig": nrm(ks[11], (L, D_RNN), 0.01),
        "lru_lambda": lru_lambda,
        "w_out_b": nrm(ks[12], (L, D_RNN, D_MODEL), D_RNN ** -0.5),
        "w_o": nrm(ks[13], (L, D_MODEL, D_MODEL), D_MODEL ** -0.5),
        "norm_ffn": 1.0 + nrm(ks[14], (L, D_MODEL), 0.01),
        "w_q": nrm(ks[15], (L, D_MODEL, PEER_HEADS * PEER_QDIM), D_MODEL ** -0.5),
        "sub_keys_1": nrm(ks[16], (L, PEER_HEADS, PEER_NKEYS, PEER_HALF), PEER_HALF ** -0.5),
        "sub_keys_2": nrm(ks[17], (L, PEER_HEADS, PEER_NKEYS, PEER_HALF), PEER_HALF ** -0.5),
        "expert_u": nrm(ks[18], (L, PEER_NEXPERTS, D_MODEL), D_MODEL ** -0.5),
        "expert_v": nrm(ks[19], (L, PEER_NEXPERTS, D_MODEL), PEER_HEADS ** -0.5),
        "norm_final": 1.0 + nrm(ks[20], (D_MODEL,), 0.01),
    }


def reference(x, norm_mix, w_in, conv_a_w, w_out_a, rnn_conv_w, rnn_conv_b,
              w_rg, b_rg, w_ig, b_ig, lru_lambda, w_out_b, w_o, norm_ffn,
              w_q, sub_keys_1, sub_keys_2, expert_u, expert_v, norm_final):
    h = x
    for l in range(DEPTH):
        n = rms_norm(h, norm_mix[l])
        h = h + hybrid_mixer(n, w_in[l], conv_a_w[l], w_out_a[l], rnn_conv_w[l],
                             rnn_conv_b[l], w_rg[l], b_rg[l], w_ig[l], b_ig[l],
                             lru_lambda[l], w_out_b[l], w_o[l])
        n = rms_norm(h, norm_ffn[l])
        h = h + peer(n, w_q[l], sub_keys_1[l], sub_keys_2[l], expert_u[l], expert_v[l])
    return rms_norm(h, norm_final)
```

```python
import functools

import jax
import jax.numpy as jnp
from jax import lax
from jax.experimental import pallas as pl
from jax.experimental.pallas import tpu as pltpu

F32 = jnp.float32
BF16 = jnp.bfloat16

EPS = 1e-6
LRU_C = 8.0
CONV_WIDTH = 3
RNN_CONV_WIDTH = 4
RNN_BLOCK = 128
PEER_HEADS = 8
PEER_NKEYS = 128
PEER_HALF = 128
PEER_TOPK = 16

LANES = 128
SUBLANES = 8
PACK_ROWS = 4
VMEM_LIMIT = 60 * 1024 * 1024

SEQ_TILE = 256
TOK_TILE = 128


def _rms(x, g):
    return x * lax.rsqrt(jnp.mean(x * x, axis=-1, keepdims=True) + EPS) * g


def _gelu_tanh(x):
    c = 0.7978845608028654
    return 0.5 * x * (1.0 + jnp.tanh(c * (x + 0.044715 * (x * x * x))))


def _sigmoid(x):
    return 1.0 / (1.0 + jnp.exp(-x))


def _shift_rows(z, prev_tail, j):
    zj = pltpu.roll(z, j, axis=0)
    pj = pltpu.roll(prev_tail, j, axis=0)
    row = lax.broadcasted_iota(jnp.int32, (SUBLANES, 1), 0)
    head = jnp.where(row < j, pj, zj[0:SUBLANES])
    return jnp.concatenate([head, zj[SUBLANES:]], axis=0)


def _causal_conv(z, prev_tail, w):
    k = w.shape[0]
    out = z * w[k - 1:k]
    for j in range(1, k):
        out = out + _shift_rows(z, prev_tail, j) * w[k - 1 - j:k - j]
    return out


def _mixer_kernel(x_ref, gmix_ref, w_in_ref, conv_a_ref, w_out_a_ref, rconv_w_ref, rconv_b_ref,
                  w_rg_ref, b_rg_ref, w_ig_ref, b_ig_ref, lam_ref, w_out_b_ref, w_o_ref,
                  h_ref, zprev_ref, rprev_ref, hprev_ref, *, d_conv, d_rnn, d_model):
    @pl.when(pl.program_id(1) == 0)
    def _():
        zprev_ref[...] = jnp.zeros_like(zprev_ref)
        rprev_ref[...] = jnp.zeros_like(rprev_ref)
        hprev_ref[...] = jnp.zeros_like(hprev_ref)

    x = x_ref[0]
    ts = x.shape[0]
    nb = _rms(x, gmix_ref[...]).astype(BF16)

    def proj(lo, width):
        return jnp.dot(nb, w_in_ref[:, lo:lo + width], preferred_element_type=F32)

    o_ab = d_conv
    o_ac = 2 * d_conv
    o_rx = 3 * d_conv
    o_ry = o_rx + d_rnn
    o_ga = o_ry + d_rnn
    o_gb = o_ga + d_model

    z = proj(o_ac, d_conv) * proj(0, d_conv)
    conv_a = _causal_conv(z, zprev_ref[...], conv_a_ref[...])
    zprev_ref[...] = z[ts - SUBLANES:]
    y_a = jnp.dot((proj(o_ab, d_conv) * conv_a).astype(BF16), w_out_a_ref[...],
                  preferred_element_type=F32)

    r_x = proj(o_rx, d_rnn)
    r_c = _causal_conv(r_x, rprev_ref[...], rconv_w_ref[...]) + rconv_b_ref[...]
    rprev_ref[...] = r_x[ts - SUBLANES:]
    r_cb = r_c.astype(BF16)
    gr, gi = [], []
    for blk in range(d_rnn // RNN_BLOCK):
        xb = r_cb[:, blk * RNN_BLOCK:(blk + 1) * RNN_BLOCK]
        gr.append(jnp.dot(xb, w_rg_ref[blk], preferred_element_type=F32))
        gi.append(jnp.dot(xb, w_ig_ref[blk], preferred_element_type=F32))
    r_gate = _sigmoid(jnp.concatenate(gr, axis=-1) + b_rg_ref[...])
    i_gate = _sigmoid(jnp.concatenate(gi, axis=-1) + b_ig_ref[...])
    neg_lam = -lam_ref[...]
    softplus = jnp.maximum(neg_lam, 0.0) + jnp.log(1.0 + jnp.exp(-jnp.abs(neg_lam)))
    log_a = (-LRU_C) * r_gate * softplus
    a = jnp.exp(log_a)
    b = jnp.sqrt(jnp.maximum(1.0 - a * a, 0.0)) * i_gate * r_c

    row = lax.broadcasted_iota(jnp.int32, (ts, 1), 0)
    s = 1
    while s < ts:
        a_s = pltpu.roll(a, s, axis=0)
        b_s = pltpu.roll(b, s, axis=0)
        valid = row >= s
        b = jnp.where(valid, a * b_s + b, b)
        a = jnp.where(valid, a * a_s, a)
        s *= 2
    h_rnn = a * hprev_ref[SUBLANES - 1:SUBLANES, :] + b
    hprev_ref[...] = h_rnn[ts - SUBLANES:]

    y_b = jnp.dot((_gelu_tanh(proj(o_ry, d_rnn)) * h_rnn).astype(BF16), w_out_b_ref[...],
                  preferred_element_type=F32)
    merged = _sigmoid(proj(o_ga, d_model)) * y_a + _sigmoid(proj(o_gb, d_model)) * y_b
    h_ref[0] = x + jnp.dot(merged.astype(BF16), w_o_ref[...], preferred_element_type=F32)


def _const_spec(shape):
    nd = len(shape)
    return pl.BlockSpec(shape, lambda *_: (0,) * nd, pipeline_mode=pl.Buffered(1))


def _mixer(x, gmix, w_in, conv_a_w, w_out_a, rconv_w, rconv_b, w_rg, b_rg, w_ig, b_ig, lam,
           w_out_b, w_o):
    bsz, seq, d_model = x.shape
    d_conv = conv_a_w.shape[1]
    d_rnn = rconv_w.shape[1]
    ts = min(SEQ_TILE, seq)
    assert seq % ts == 0
    consts = (gmix, w_in, conv_a_w, w_out_a, rconv_w, rconv_b, w_rg, b_rg, w_ig, b_ig, lam,
              w_out_b, w_o)
    kern = functools.partial(_mixer_kernel, d_conv=d_conv, d_rnn=d_rnn, d_model=d_model)
    return pl.pallas_call(
        kern,
        grid=(bsz, seq // ts),
        in_specs=[pl.BlockSpec((1, ts, d_model), lambda b, j: (b, j, 0))]
                 + [_const_spec(c.shape) for c in consts],
        out_specs=pl.BlockSpec((1, ts, d_model), lambda b, j: (b, j, 0)),
        out_shape=jax.ShapeDtypeStruct(x.shape, F32),
        scratch_shapes=[pltpu.VMEM((SUBLANES, d_conv), F32),
                        pltpu.VMEM((SUBLANES, d_rnn), F32),
                        pltpu.VMEM((SUBLANES, d_rnn), F32)],
        compiler_params=pltpu.CompilerParams(
            dimension_semantics=("arbitrary", "arbitrary"), vmem_limit_bytes=VMEM_LIMIT),
        name="mixer",
    )(x, *consts)


def _top16(s, payload=None):
    rows = s.shape[0]
    rid = lax.broadcasted_iota(jnp.int32, (rows, 1), 0)
    kid = lax.broadcasted_iota(jnp.int32, (PEER_TOPK, 1), 0)
    vals = jnp.zeros((PEER_TOPK, s.shape[1]), F32)
    sel = jnp.zeros((PEER_TOPK, s.shape[1]), jnp.int32)
    val_rows, sel_rows = [], []
    for k in range(PEER_TOPK):
        m = jnp.max(s, axis=0, keepdims=True)
        im = jnp.min(jnp.where(s == m, rid, rows), axis=0, keepdims=True)
        hit = rid == im
        if payload is None:
            out = im
        else:
            out = jnp.max(jnp.where(hit, payload, -1), axis=0, keepdims=True)
        s = jnp.where(hit, -jnp.inf, s)
        vals = jnp.where(kid == k, m, vals)
        sel = jnp.where(kid == k, out, sel)
        val_rows.append(m)
        sel_rows.append(out)
    return vals, sel, val_rows, sel_rows


def _peer_select_kernel(h_ref, gffn_ref, wq_ref, k1_ref, k2_ref, n2_ref, idx_ref, gate_ref):
    n2 = _rms(h_ref[...], gffn_ref[...])
    n2_ref[...] = n2
    q = jnp.dot(n2.astype(BF16), wq_ref[...], preferred_element_type=F32).astype(BF16)
    nt = (((1,), (1,)), ((), ()))
    idx_parts, gate_parts = [], []
    for hd in range(PEER_HEADS):
        base = hd * 2 * PEER_HALF
        s1 = lax.dot_general(k1_ref[hd], q[:, base:base + PEER_HALF], nt,
                             preferred_element_type=F32)
        s2 = lax.dot_general(k2_ref[hd], q[:, base + PEER_HALF:base + 2 * PEER_HALF], nt,
                             preferred_element_type=F32)
        _, _, v1_rows, i1_rows = _top16(s1)
        v2, i2, _, _ = _top16(s2)
        cand_s = jnp.concatenate([v1_rows[a] + v2 for a in range(PEER_TOPK)], axis=0)
        cand_i = jnp.concatenate([i1_rows[a] * PEER_NKEYS + i2 for a in range(PEER_TOPK)],
                                 axis=0)
        top_s, top_i, _, _ = _top16(cand_s, payload=cand_i)
        e = jnp.exp(top_s - top_s[0:1])
        gate_parts.append(e / jnp.sum(e, axis=0, keepdims=True))
        idx_parts.append(top_i * PACK_ROWS)
    gate_ref[...] = jnp.concatenate(gate_parts, axis=0).T
    idx_f = lax.bitcast_convert_type(jnp.concatenate(idx_parts, axis=0), F32)
    idx_ref[...] = lax.bitcast_convert_type(idx_f.T, jnp.int32)


def _peer_select(h1, gffn, w_q, k1, k2):
    t, d = h1.shape
    tb = TOK_TILE
    nsel = PEER_HEADS * PEER_TOPK
    return pl.pallas_call(
        _peer_select_kernel,
        grid=(t // tb,),
        in_specs=[pl.BlockSpec((tb, d), lambda i: (i, 0)),
                  _const_spec(gffn.shape), _const_spec(w_q.shape),
                  _const_spec(k1.shape), _const_spec(k2.shape)],
        out_specs=[pl.BlockSpec((tb, d), lambda i: (i, 0)),
                   pl.BlockSpec((tb, nsel), lambda i: (i, 0)),
                   pl.BlockSpec((tb, nsel), lambda i: (i, 0))],
        out_shape=[jax.ShapeDtypeStruct((t, d), F32),
                   jax.ShapeDtypeStruct((t, nsel), jnp.int32),
                   jax.ShapeDtypeStruct((t, nsel), F32)],
        compiler_params=pltpu.CompilerParams(
            dimension_semantics=("arbitrary",), vmem_limit_bytes=VMEM_LIMIT),
        name="peer_select",
    )(h1, gffn, w_q, k1, k2)


def _unpack(packed):
    hi = lax.bitcast_convert_type(packed & jnp.uint32(0xFFFF0000), F32)
    lo = lax.bitcast_convert_type(packed << 16, F32)
    return hi, lo


def _peer_act_kernel(idx_ref, xr_ref, gate_ref, tab_ref, w_ref, prod_ref, act_ref):
    tb, nsel = gate_ref.shape
    lane = lax.broadcasted_iota(jnp.int32, (1, tb), 1)

    def token(t, carry):
        xv = xr_ref[pl.ds(pl.multiple_of(t * SUBLANES, SUBLANES), SUBLANES), :]
        xh = xv[0:PACK_ROWS]
        xl = xv[PACK_ROWS:]
        for k in range(nsel):
            r = pl.multiple_of(idx_ref[t, k], PACK_ROWS)
            hi, lo = _unpack(tab_ref[pl.ds(r, PACK_ROWS), :])
            prod_ref[k * PACK_ROWS:(k + 1) * PACK_ROWS, :] = hi * xh + lo * xl
        part = prod_ref[pl.ds(0, nsel, stride=PACK_ROWS), :]
        for c in range(1, PACK_ROWS):
            part = part + prod_ref[pl.ds(c, nsel, stride=PACK_ROWS), :]
        col = jnp.sum(part, axis=-1, keepdims=True)
        act_ref[...] = jnp.where(lane == t, col, act_ref[...])
        return carry

    lax.fori_loop(0, tb, token, 0)
    w_ref[...] = gate_ref[...] * _gelu_tanh(act_ref[...].T)


def _peer_act(idx, xr, gate, table):
    t, nsel = idx.shape
    tb = TOK_TILE
    return pl.pallas_call(
        _peer_act_kernel,
        grid=(t // tb,),
        in_specs=[pl.BlockSpec((tb, nsel), lambda i: (i, 0), memory_space=pltpu.SMEM),
                  pl.BlockSpec((tb * SUBLANES, LANES), lambda i: (i, 0)),
                  pl.BlockSpec((tb, nsel), lambda i: (i, 0)),
                  _const_spec(table.shape)],
        out_specs=pl.BlockSpec((tb, nsel), lambda i: (i, 0)),
        out_shape=jax.ShapeDtypeStruct((t, nsel), F32),
        scratch_shapes=[pltpu.VMEM((nsel * PACK_ROWS, LANES), F32),
                        pltpu.VMEM((nsel, tb), F32)],
        compiler_params=pltpu.CompilerParams(
            dimension_semantics=("arbitrary",), vmem_limit_bytes=VMEM_LIMIT),
        name="peer_act",
    )(idx, xr, gate, table)


def _peer_out_kernel(idx_ref, w_ref, tab_ref, y_ref):
    tb, nsel = idx_ref.shape

    def token(t, carry):
        acc = [[jnp.zeros((PACK_ROWS, LANES), F32) for _ in range(2)] for _ in range(2)]
        for k in range(nsel):
            r = pl.multiple_of(idx_ref[t, k], PACK_ROWS)
            wk = w_ref[t, k]
            hi, lo = _unpack(tab_ref[pl.ds(r, PACK_ROWS), :])
            acc[k % 2][0] = acc[k % 2][0] + wk * hi
            acc[k % 2][1] = acc[k % 2][1] + wk * lo
        y = jnp.concatenate([acc[0][0] + acc[1][0], acc[0][1] + acc[1][1]], axis=0)
        y_ref[pl.ds(pl.multiple_of(t * SUBLANES, SUBLANES), SUBLANES), :] = y
        return carry

    lax.fori_loop(0, tb, token, 0)


def _peer_out(idx, w, table):
    t, nsel = idx.shape
    tb = TOK_TILE
    return pl.pallas_call(
        _peer_out_kernel,
        grid=(t // tb,),
        in_specs=[pl.BlockSpec((tb, nsel), lambda i: (i, 0), memory_space=pltpu.SMEM),
                  pl.BlockSpec((tb, nsel), lambda i: (i, 0), memory_space=pltpu.SMEM),
                  _const_spec(table.shape)],
        out_specs=pl.BlockSpec((tb * SUBLANES, LANES), lambda i: (i, 0)),
        out_shape=jax.ShapeDtypeStruct((t * SUBLANES, LANES), F32),
        compiler_params=pltpu.CompilerParams(
            dimension_semantics=("arbitrary",), vmem_limit_bytes=VMEM_LIMIT),
        name="peer_out",
    )(idx, w, table)


def _final_kernel(h_ref, y_ref, g_ref, o_ref):
    o_ref[...] = _rms(h_ref[...] + y_ref[...], g_ref[...])


def _final(h1, y, g):
    t, d = h1.shape
    tb = 512 if t % 512 == 0 else TOK_TILE
    return pl.pallas_call(
        _final_kernel,
        grid=(t // tb,),
        in_specs=[pl.BlockSpec((tb, d), lambda i: (i, 0)),
                  pl.BlockSpec((tb, d), lambda i: (i, 0)),
                  _const_spec(g.shape)],
        out_specs=pl.BlockSpec((tb, d), lambda i: (i, 0)),
        out_shape=jax.ShapeDtypeStruct((t, d), F32),
        compiler_params=pltpu.CompilerParams(dimension_semantics=("arbitrary",)),
        name="final_norm",
    )(h1, y, g)


def _pack_table(tab):
    n, d = tab.shape
    assert d == 2 * PACK_ROWS * LANES
    bits = lax.bitcast_convert_type(tab.astype(BF16), jnp.uint16).astype(jnp.uint32)
    packed = (bits[:, :d // 2] << 16) | bits[:, d // 2:]
    return packed.reshape(n * PACK_ROWS, LANES)


def kernel(x, norm_mix, w_in, conv_a_w, w_out_a, rnn_conv_w, rnn_conv_b, w_rg, b_rg, w_ig, b_ig,
           lru_lambda, w_out_b, w_o, norm_ffn, w_q, sub_keys_1, sub_keys_2, expert_u, expert_v,
           norm_final):
    bsz, seq, d = x.shape
    assert norm_mix.shape[0] == 1, "single layer"
    row = lambda v: v.reshape(1, -1)
    h1 = _mixer(x, row(norm_mix[0]), w_in[0].astype(BF16), conv_a_w[0], w_out_a[0].astype(BF16),
                rnn_conv_w[0], row(rnn_conv_b[0]), w_rg[0].astype(BF16), row(b_rg[0]),
                w_ig[0].astype(BF16), row(b_ig[0]), row(lru_lambda[0]), w_out_b[0].astype(BF16),
                w_o[0].astype(BF16))
    h1 = h1.reshape(bsz * seq, d)
    n2, idx, gate = _peer_select(h1, row(norm_ffn[0]), w_q[0].astype(BF16),
                                 sub_keys_1[0].astype(BF16), sub_keys_2[0].astype(BF16))
    xr = n2.reshape(bsz * seq * SUBLANES, LANES)
    w = _peer_act(idx, xr, gate, _pack_table(expert_u[0]))
    y = _peer_out(idx, w, _pack_table(expert_v[0]))
    out = _final(h1, y.reshape(bsz * seq, d), row(norm_final))
    return out.reshape(bsz, seq, d)
```
